```python
import jax, jax.numpy as jnp
from jax import lax
import numpy as np

D_MODEL = 2048
BATCH = 1
SEQ = 16384
DEPTH = 1

CHUNK = 64
D_MIX = D_MODEL
D_RWKV = D_MIX // 2
RWKV_HEAD = 64
RWKV_HEADS = D_RWKV // RWKV_HEAD
D_CONV = D_MIX - D_RWKV
CONV_WIDTH = 3
D_IN = 3 * D_RWKV + 3 * D_CONV
LORA_DECAY = 64
LORA_AAA = 64
LORA_GATE = 160
GN_EPS = 64e-5
RMS_EPS = 1e-6
PEER_HEADS = 8
PEER_NKEYS = 128
PEER_EXPERTS = PEER_NKEYS * PEER_NKEYS
PEER_QDIM = 512
PEER_HALF = PEER_QDIM // 2
PEER_TOPK = 16
PEER_BLOCK = 128

kernel_name = "hymba_rwkv7_shortconv_peer_block"


def rms_norm(x, gain):
    xf = x.astype(jnp.float32)
    y = xf * lax.rsqrt(jnp.mean(xf * xf, axis=-1, keepdims=True) + RMS_EPS)
    return (y * gain.astype(jnp.float32)).astype(x.dtype)


def token_shift(t):
    return jnp.pad(t, ((0, 0), (1, 0), (0, 0)))[:, :-1]


def rwkv7_scan(r, w, k, v, kk, a):
    B, S, H, N = r.shape
    nc = S // CHUNK

    def to_chunks(t):
        return jnp.moveaxis(t.reshape(B, nc, CHUNK, H, N), (1, 2), (0, 1))

    def step(state, inp):
        r_t, w_t, k_t, v_t, kk_t, a_t = inp
        sa = jnp.einsum('bhvk,bhk->bhv', state, -kk_t)
        state = (state * w_t[:, :, None, :]
                 + sa[..., None] * (kk_t * a_t)[:, :, None, :]
                 + v_t[..., None] * k_t[:, :, None, :])
        y_t = jnp.einsum('bhvk,bhk->bhv', state, r_t)
        return state, y_t

    def chunk_step(state, chunk_inp):
        return lax.scan(step, state, chunk_inp)

    state0 = jnp.zeros((B, H, N, N), jnp.float32)
    _, y = lax.scan(chunk_step, state0, tuple(to_chunks(t) for t in (r, w, k, v, kk, a)))
    return jnp.moveaxis(y, (0, 1), (1, 2)).reshape(B, S, H, N)


def causal_depthwise_conv(u, conv_w):
    S = u.shape[1]
    up = jnp.pad(u, ((0, 0), (CONV_WIDTH - 1, 0), (0, 0)))
    out = up[:, 0:S] * conv_w[0]
    for j in range(1, CONV_WIDTH):
        out = out + up[:, j:j + S] * conv_w[j]
    return out


def peer_ffn(h, wq, keys, u_tab, v_tab):
    B, S, D = h.shape
    q = (h @ wq).reshape(B, S, PEER_HEADS, 2, PEER_HALF)
    s = jnp.einsum('bshcd,hcnd->bshcn', q, keys).astype(jnp.float32)
    top_s, top_i = lax.top_k(s, PEER_TOPK)
    cand = top_s[..., 0, :, None] + top_s[..., 1, None, :]
    cand_s, cand_i = lax.top_k(cand.reshape(B, S, PEER_HEADS, PEER_TOPK * PEER_TOPK), PEER_TOPK)
    e1 = jnp.take_along_axis(top_i[..., 0, :], cand_i // PEER_TOPK, axis=-1)
    e2 = jnp.take_along_axis(top_i[..., 1, :], cand_i % PEER_TOPK, axis=-1)
    expert = e1 * PEER_NKEYS + e2
    gates = jax.nn.softmax(cand_s, axis=-1).astype(h.dtype)

    T = B * S
    nb = T // PEER_BLOCK
    hk = PEER_HEADS * PEER_TOPK
    xb = h.reshape(nb, PEER_BLOCK, D)
    ib = expert.reshape(nb, PEER_BLOCK, hk)
    gb = gates.reshape(nb, PEER_BLOCK, hk)

    def block(args):
        xt, it, gt = args
        u = jnp.take(u_tab, it, axis=0)
        act = jax.nn.gelu(jnp.einsum('pd,pkd->pk', xt, u)) * gt
        vv = jnp.take(v_tab, it, axis=0)
        return jnp.einsum('pk,pkd->pd', act, vv)

    out = lax.map(block, (xb, ib, gb))
    return out.reshape(B, S, D)


def setup_inputs(seed: int = 0) -> dict:
    key = jax.random.key(seed)
    ks = jax.random.split(key, 32)
    f32 = jnp.float32
    L, D = DEPTH, D_MODEL

    def nrm(k, shape, scale):
        return jax.random.normal(k, shape, f32) * scale

    x = jax.random.normal(ks[0], (BATCH, SEQ, D), f32)
    norm_mix = 1.0 + nrm(ks[1], (L, D), 0.02)
    w_in = nrm(ks[2], (L, D, D_IN), D ** -0.5)
    mu_rkv = jax.random.uniform(ks[3], (L, 3 * D_RWKV), f32)
    mu_lora = jax.random.uniform(ks[4], (L, 3, D), f32)
    w0 = -3.0 + nrm(ks[5], (L, D_RWKV), 1.0)
    w1 = nrm(ks[6], (L, D, LORA_DECAY), D ** -0.5)
    w2 = nrm(ks[7], (L, LORA_DECAY, D_RWKV), 0.5 * LORA_DECAY ** -0.5)
    a0 = nrm(ks[8], (L, D_RWKV), 0.5)
    a1 = nrm(ks[9], (L, D, LORA_AAA), D ** -0.5)
    a2 = nrm(ks[10], (L, LORA_AAA, D_RWKV), 0.5 * LORA_AAA ** -0.5)
    g1 = nrm(ks[11], (L, D, LORA_GATE), D ** -0.5)
    g2 = nrm(ks[12], (L, LORA_GATE, D_RWKV), LORA_GATE ** -0.5)
    k_k = 0.85 + nrm(ks[13], (L, D_RWKV), 0.05)
    k_a = 1.0 + nrm(ks[14], (L, D_RWKV), 0.05)
    r_k = nrm(ks[15], (L, RWKV_HEADS, RWKV_HEAD), 0.1)
    gn_w = 1.0 + nrm(ks[16], (L, D_RWKV), 0.02)
    gn_b = nrm(ks[17], (L, D_RWKV), 0.02)
    conv_w = nrm(ks[18], (L, CONV_WIDTH, D_CONV), CONV_WIDTH ** -0.5)
    w_out = nrm(ks[19], (L, D_MIX, D), D_MIX ** -0.5)
    norm_ffn = 1.0 + nrm(ks[20], (L, D), 0.02)
    peer_wq = nrm(ks[21], (L, D, PEER_HEADS * PEER_QDIM), D ** -0.5)
    peer_keys = nrm(ks[22], (L, PEER_HEADS, 2, PEER_NKEYS, PEER_HALF), PEER_HALF ** -0.5)
    peer_u = nrm(ks[23], (L, PEER_EXPERTS, D), D ** -0.5)
    peer_v = nrm(ks[24], (L, PEER_EXPERTS, D), 0.05)
    norm_final = 1.0 + nrm(ks[25], (D,), 0.02)
    return {"x": x, "norm_mix": norm_mix, "w_in": w_in, "mu_rkv": mu_rkv, "mu_lora": mu_lora,
            "w0": w0, "w1": w1, "w2": w2, "a0": a0, "a1": a1, "a2": a2, "g1": g1, "g2": g2,
            "k_k": k_k, "k_a": k_a, "r_k": r_k, "gn_w": gn_w, "gn_b": gn_b, "conv_w": conv_w,
            "w_out": w_out, "norm_ffn": norm_ffn, "peer_wq": peer_wq, "peer_keys": peer_keys,
            "peer_u": peer_u, "peer_v": peer_v, "norm_final": norm_final}


def reference(x, norm_mix, w_in, mu_rkv, mu_lora, w0, w1, w2, a0, a1, a2, g1, g2,
              k_k, k_a, r_k, gn_w, gn_b, conv_w, w_out, norm_ffn, peer_wq, peer_keys,
              peer_u, peer_v, norm_final):
    B, S, _ = x.shape
    H, N = RWKV_HEADS, RWKV_HEAD
    f32 = jnp.float32
    for l in range(DEPTH):
        h = rms_norm(x, norm_mix[l])
        proj = h @ w_in[l]
        rkv = proj[..., :3 * D_RWKV]
        rkv = rkv + (token_shift(rkv) - rkv) * mu_rkv[l]
        r, k, v = jnp.split(rkv, 3, axis=-1)
        b_gate, c_gate, xc = jnp.split(proj[..., 3 * D_RWKV:], 3, axis=-1)

        dh = token_shift(h) - h
        xw = h + dh * mu_lora[l, 0]
        xa = h + dh * mu_lora[l, 1]
        xg = h + dh * mu_lora[l, 2]
        w_log = -jax.nn.softplus(-(w0[l] + jnp.tanh(xw @ w1[l]) @ w2[l])) - 0.5
        decay = jnp.exp(-jnp.exp(w_log.astype(f32)))
        a = jax.nn.sigmoid(a0[l] + (xa @ a1[l]) @ a2[l])
        g = jax.nn.sigmoid(xg @ g1[l]) @ g2[l]

        kk = (k * k_k[l]).astype(f32).reshape(B, S, H, N)
        kk = kk / jnp.maximum(jnp.sqrt(jnp.sum(kk * kk, axis=-1, keepdims=True)), 1e-12)
        k = k * (1.0 + (a - 1.0) * k_a[l])

        rh = r.astype(f32).reshape(B, S, H, N)
        kh = k.astype(f32).reshape(B, S, H, N)
        vh = v.astype(f32).reshape(B, S, H, N)
        ah = a.astype(f32).reshape(B, S, H, N)
        y = rwkv7_scan(rh, decay.reshape(B, S, H, N), kh, vh, kk, ah)

        mean = jnp.mean(y, axis=-1, keepdims=True)
        var = jnp.mean(jnp.square(y - mean), axis=-1, keepdims=True)
        y = (y - mean) * lax.rsqrt(var + GN_EPS)
        y = y * gn_w[l].reshape(H, N).astype(f32) + gn_b[l].reshape(H, N).astype(f32)
        y = y + jnp.sum(rh * kh * r_k[l].astype(f32), axis=-1, keepdims=True) * vh
        y_rwkv = (y.reshape(B, S, D_RWKV) * g.astype(f32)).astype(x.dtype)

        y_conv = b_gate * causal_depthwise_conv(c_gate * xc, conv_w[l])

        mix = jnp.concatenate([y_rwkv, y_conv], axis=-1) @ w_out[l]
        x = x + mix

        h2 = rms_norm(x, norm_ffn[l])
        x = x + peer_ffn(h2, peer_wq[l], peer_keys[l], peer_u[l], peer_v[l])
    return rms_norm(x, norm_final)
```

```python
import functools

import jax
import jax.numpy as jnp
from jax import lax
from jax.experimental import pallas as pl
from jax.experimental.pallas import tpu as pltpu

F32 = jnp.float32
BF16 = jnp.bfloat16
I32 = jnp.int32

LANES = 128
RWKV_HEAD = 64
HEADS_PER_GROUP = LANES // RWKV_HEAD
SCAN_CHUNK = 64
LORA_DECAY = 64
LORA_AAA = 64
LORA_GATE = 160
CONV_WIDTH = 3
RMS_EPS = 1e-6
GN_EPS = 64e-5
PEER_HEADS = 8
PEER_NKEYS = 128
PEER_HALF = 256
PEER_TOPK = 16
VMEM_LIMIT = 56 * 1024 * 1024

NT_DIMS = (((1,), (1,)), ((), ()))
TN_DIMS = (((0,), (0,)), ((), ()))
NEG_INF = float("-inf")


def _dot(a, b):
    return jnp.dot(a, b, preferred_element_type=F32)


def _dot_nt(a, b):
    return lax.dot_general(a, b, NT_DIMS, preferred_element_type=F32)


def _dot_tn(a, b):
    return lax.dot_general(a, b, TN_DIMS, preferred_element_type=F32)


def _split3(t):
    hi = t.astype(BF16)
    r1 = t - hi.astype(F32)
    mid = r1.astype(BF16)
    lo = (r1 - mid.astype(F32)).astype(BF16)
    return hi, mid, lo


def _head_sum(t, ones_bd):
    hi, mid, lo = _split3(t)
    return _dot(hi, ones_bd) + _dot(mid, ones_bd) + _dot(lo, ones_bd)


def _block_diag_ones(n):
    r = lax.broadcasted_iota(I32, (n, n), 0) >> 6
    c = lax.broadcasted_iota(I32, (n, n), 1) >> 6
    return (r == c).astype(BF16)


def _shift_rows(t, carry, n):
    rolled = pltpu.roll(t, shift=n, axis=0)
    row = lax.broadcasted_iota(I32, t.shape, 0)
    out = jnp.where(row == 0, carry[8 - n:9 - n, :], rolled)
    if n == 2:
        out = jnp.where(row == 1, carry[7:8, :], out)
    return out


def _mix_in_kernel(x_ref, nm_ref, mul_ref, w1_ref, a1_ref, g1_ref,
                   wr_ref, wk_ref, wv_ref, wb_ref, wc_ref, wx_ref,
                   mur_ref, muk_ref, muv_ref, w0_ref, w2_ref, a0_ref, a2_ref, g2_ref,
                   kkw_ref, kaw_ref, cw_ref,
                   r_out, lw_out, k_out, v_out, kk_out, a_out, g_out, yc_out,
                   h_s, hw_s, ha_s, hg_s, hlast_s, plast_s, ulast_s):
    i = pl.program_id(0)
    j = pl.program_id(1)
    tm = x_ref.shape[0]

    @pl.when(i == 0)
    def _():
        plast_s[j] = jnp.zeros(plast_s.shape[1:], F32)
        ulast_s[j] = jnp.zeros(ulast_s.shape[1:], F32)

    @pl.when(jnp.logical_and(i == 0, j == 0))
    def _():
        hlast_s[...] = jnp.zeros(hlast_s.shape, F32)

    @pl.when(j == 0)
    def _():
        x = x_ref[...]
        h = x * lax.rsqrt(jnp.mean(x * x, axis=-1, keepdims=True) + RMS_EPS) * nm_ref[...]
        h_s[...] = h.astype(BF16)
        dh = _shift_rows(h, hlast_s[...], 1) - h
        hlast_s[...] = h[tm - 8:, :]
        xw = (h + dh * mul_ref[0:1, :]).astype(BF16)
        xa = (h + dh * mul_ref[1:2, :]).astype(BF16)
        xg = (h + dh * mul_ref[2:3, :]).astype(BF16)
        hw_s[...] = jnp.tanh(_dot(xw, w1_ref[...])).astype(BF16)
        ha_s[...] = _dot(xa, a1_ref[...]).astype(BF16)
        hg_s[...] = jax.nn.sigmoid(_dot(xg, g1_ref[...])).astype(BF16)

    hb = h_s[...]
    tc = wr_ref.shape[1]
    plast = plast_s[j]

    def shifted_proj(w_ref, mu_ref, slot):
        p = _dot(hb, w_ref[...])
        prev = _shift_rows(p, plast[:, slot * tc:(slot + 1) * tc], 1)
        plast_s[j, :, slot * tc:(slot + 1) * tc] = p[tm - 8:, :]
        return p + (prev - p) * mu_ref[...]

    r = shifted_proj(wr_ref, mur_ref, 0)
    k = shifted_proj(wk_ref, muk_ref, 1)
    v = shifted_proj(wv_ref, muv_ref, 2)

    u = _dot(hb, wc_ref[...]) * _dot(hb, wx_ref[...])
    ulast = ulast_s[j]
    conv = (_shift_rows(u, ulast, 2) * cw_ref[0:1, :] + _shift_rows(u, ulast, 1) * cw_ref[1:2, :]
            + u * cw_ref[2:3, :])
    ulast_s[j] = u[tm - 8:, :]
    yc_out[...] = (_dot(hb, wb_ref[...]) * conv).astype(yc_out.dtype)

    z = w0_ref[...] + _dot(hw_s[...], w2_ref[...])
    w_log = -jax.nn.softplus(-z) - 0.5
    lw_out[...] = -jnp.exp(w_log)
    a = jax.nn.sigmoid(a0_ref[...] + _dot(ha_s[...], a2_ref[...]))
    g_out[...] = _dot(hg_s[...], g2_ref[...])

    kk = k * kkw_ref[...]
    ss = _head_sum(kk * kk, _block_diag_ones(tc))
    kk_out[...] = kk / jnp.maximum(jnp.sqrt(ss), 1e-12)
    k_out[...] = k * (1.0 + (a - 1.0) * kaw_ref[...])
    r_out[...] = r
    v_out[...] = v
    a_out[...] = a


def _mix_in(x2, norm_mix, w_in, mu_rkv, mu_lora, w0, w1, w2, a0, a1, a2, g1, g2, k_k, k_a, conv_w,
            *, tm, tc):
    S, D = x2.shape
    C = w0.shape[-1]
    nj = C // tc
    wb = w_in.astype(BF16)
    row = lambda t: t.reshape(1, -1).astype(F32)

    def wspec(slot):
        return pl.BlockSpec((D, tc), lambda i, j, slot=slot: (0, slot * nj + j))

    def cspec(rows=1):
        return pl.BlockSpec((rows, tc), lambda i, j: (0, j))

    def muspec(slot):
        return pl.BlockSpec((1, tc), lambda i, j, slot=slot: (0, slot * nj + j))

    full = lambda shape: pl.BlockSpec(shape, lambda i, j: (0,) * len(shape))
    mu_row = row(mu_rkv)
    in_specs = [
        pl.BlockSpec((tm, D), lambda i, j: (i, 0)),
        full((1, D)),
        full((3, D)),
        full((D, LORA_DECAY)), full((D, LORA_AAA)), full((D, LORA_GATE)),
        wspec(0), wspec(1), wspec(2), wspec(3), wspec(4), wspec(5),
        muspec(0), muspec(1), muspec(2),
        cspec(), pl.BlockSpec((LORA_DECAY, tc), lambda i, j: (0, j)),
        cspec(), pl.BlockSpec((LORA_AAA, tc), lambda i, j: (0, j)),
        pl.BlockSpec((LORA_GATE, tc), lambda i, j: (0, j)),
        cspec(), cspec(), cspec(CONV_WIDTH),
    ]
    out_spec = pl.BlockSpec((tm, tc), lambda i, j: (i, j))
    f32_out = jax.ShapeDtypeStruct((S, C), F32)
    outs = pl.pallas_call(
        _mix_in_kernel,
        grid=(S // tm, nj),
        in_specs=in_specs,
        out_specs=[out_spec] * 8,
        out_shape=[f32_out] * 7 + [jax.ShapeDtypeStruct((S, C), BF16)],
        scratch_shapes=[
            pltpu.VMEM((tm, D), BF16),
            pltpu.VMEM((tm, LORA_DECAY), BF16),
            pltpu.VMEM((tm, LORA_AAA), BF16),
            pltpu.VMEM((tm, LORA_GATE), BF16),
            pltpu.VMEM((8, D), F32),
            pltpu.VMEM((nj, 8, 3 * tc), F32),
            pltpu.VMEM((nj, 8, tc), F32),
        ],
        compiler_params=pltpu.CompilerParams(
            dimension_semantics=("arbitrary", "arbitrary"), vmem_limit_bytes=VMEM_LIMIT),
        name="mix_in",
    )(x2, row(norm_mix), mu_lora.astype(F32), w1.astype(BF16), a1.astype(BF16), g1.astype(BF16),
      wb, wb, wb, wb, wb, wb, mu_row, mu_row, mu_row,
      row(w0), w2.astype(BF16), row(a0), a2.astype(BF16), g2.astype(BF16),
      row(k_k), row(k_a), conv_w.astype(F32))
    return outs


def _rwkv_kernel(r_ref, lw_ref, k_ref, v_ref, kk_ref, a_ref, g_ref, gnw_ref, gnb_ref, rk_ref,
                 y_ref, s_ref):
    c = pl.program_id(0)
    T = SCAN_CHUNK
    G = LANES

    @pl.when(c == 0)
    def _():
        s_ref[...] = jnp.zeros(s_ref.shape, F32)

    head0 = lax.broadcasted_iota(I32, (T, G), 1) < RWKV_HEAD
    row = lax.broadcasted_iota(I32, (2 * T, 2 * T), 0)
    col = lax.broadcasted_iota(I32, (2 * T, 2 * T), 1)
    strict = (col & (T - 1)) < (row & (T - 1))
    incl = (col & (T - 1)) <= (row & (T - 1))
    eye = (row == col).astype(F32)
    ones_bd = _block_diag_ones(G)
    tri = (lax.broadcasted_iota(I32, (T, T), 0) >= lax.broadcasted_iota(I32, (T, T), 1)).astype(BF16)

    def stack(t):
        return jnp.concatenate([jnp.where(head0, t, 0.0), jnp.where(head0, 0.0, t)], axis=0)

    for p in range(r_ref.shape[1] // G):
        sl = slice(p * G, (p + 1) * G)
        lw = lw_ref[:, sl]
        hi, mid, lo = _split3(lw)
        cs = _dot(tri, hi) + _dot(tri, mid) + _dot(tri, lo)
        cs_end = cs[T - 1:T, :]
        decay_in = jnp.exp(cs)
        decay_prev = jnp.exp(cs - lw)
        grow = jnp.exp(-cs)
        decay_out = jnp.exp(cs_end - cs)
        kk = kk_ref[:, sl]
        beta = kk * a_ref[:, sl]
        rv = r_ref[:, sl]
        kv = k_ref[:, sl]
        vv = v_ref[:, sl]

        xa = stack(-kk * decay_prev).astype(BF16)
        xr = stack(rv * decay_in).astype(BF16)
        yb = stack(beta * grow).astype(BF16)
        yk = stack(kv * grow).astype(BF16)
        vs = stack(vv).astype(BF16)

        l_ab = jnp.where(strict, _dot_nt(xa, yb), 0.0)
        l_ak = jnp.where(strict, _dot_nt(xa, yk), 0.0)
        m_rb = jnp.where(incl, _dot_nt(xr, yb), 0.0)
        m_rk = jnp.where(incl, _dot_nt(xr, yk), 0.0)

        npow = l_ab
        tinv = eye + l_ab
        for _ in range(5):
            nb = npow.astype(BF16)
            npow = _dot(nb, nb)
            tinv = tinv + _dot(tinv.astype(BF16), npow.astype(BF16))

        s0 = s_ref[p]
        s0b = s0.astype(BF16)
        rhs = _dot(l_ak.astype(BF16), vs) + _dot_nt(xa, s0b)
        u = _dot(tinv.astype(BF16), rhs.astype(BF16))
        ub = u.astype(BF16)
        yst = _dot(m_rb.astype(BF16), ub) + _dot(m_rk.astype(BF16), vs) + _dot_nt(xr, s0b)
        y = yst[:T, :] + yst[T:, :]

        ybo = stack(beta * decay_out).astype(BF16)
        yko = stack(kv * decay_out).astype(BF16)
        s_ref[p] = s0 * jnp.exp(cs_end) + _dot_tn(ub, ybo) + _dot_tn(vs, yko)

        inv_n = 1.0 / RWKV_HEAD
        mean = _head_sum(y, ones_bd) * inv_n
        d = y - mean
        var = _head_sum(d * d, ones_bd) * inv_n
        yn = d * lax.rsqrt(var + GN_EPS) * gnw_ref[:, sl] + gnb_ref[:, sl]
        bonus = _head_sum(rv * kv * rk_ref[:, sl], ones_bd) * vv
        y_ref[:, sl] = ((yn + bonus) * g_ref[:, sl]).astype(y_ref.dtype)


def _rwkv_scan(r, lw, k, v, kk, a, g, gn_w, gn_b, r_k):
    S, C = r.shape
    blk = pl.BlockSpec((SCAN_CHUNK, C), lambda c: (c, 0))
    par = pl.BlockSpec((1, C), lambda c: (0, 0))
    row = lambda t: t.reshape(1, -1).astype(F32)
    return pl.pallas_call(
        _rwkv_kernel,
        grid=(S // SCAN_CHUNK,),
        in_specs=[blk] * 7 + [par] * 3,
        out_specs=blk,
        out_shape=jax.ShapeDtypeStruct((S, C), BF16),
        scratch_shapes=[pltpu.VMEM((C // LANES, LANES, LANES), F32)],
        compiler_params=pltpu.CompilerParams(
            dimension_semantics=("arbitrary",), vmem_limit_bytes=VMEM_LIMIT),
        name="rwkv_scan",
    )(r, lw, k, v, kk, a, g, row(gn_w), row(gn_b), row(r_k))


def _mix_out_kernel(x_ref, ya_ref, yb_ref, wa_ref, wb_ref, nf_ref, x1_out, h2_out):
    x1 = x_ref[...] + _dot(ya_ref[...], wa_ref[...]) + _dot(yb_ref[...], wb_ref[...])
    x1_out[...] = x1
    h2 = x1 * lax.rsqrt(jnp.mean(x1 * x1, axis=-1, keepdims=True) + RMS_EPS) * nf_ref[...]
    h2_out[...] = h2.astype(h2_out.dtype)


def _mix_out(x2, y_rwkv, y_conv, w_out, norm_ffn, *, tm):
    S, D = x2.shape
    C = y_rwkv.shape[1]
    wo = w_out.astype(BF16)
    rows = lambda n: pl.BlockSpec((tm, n), lambda i: (i, 0))
    return pl.pallas_call(
        _mix_out_kernel,
        grid=(S // tm,),
        in_specs=[rows(D), rows(C), rows(C),
                  pl.BlockSpec((C, D), lambda i: (0, 0)), pl.BlockSpec((C, D), lambda i: (1, 0)),
                  pl.BlockSpec((1, D), lambda i: (0, 0))],
        out_specs=[rows(D), rows(D)],
        out_shape=[jax.ShapeDtypeStruct((S, D), F32), jax.ShapeDtypeStruct((S, D), BF16)],
        compiler_params=pltpu.CompilerParams(
            dimension_semantics=("arbitrary",), vmem_limit_bytes=VMEM_LIMIT),
        name="mix_out",
    )(x2, y_rwkv, y_conv, wo, wo, norm_ffn.reshape(1, -1).astype(F32))


def _top16(s):
    n = lax.broadcasted_iota(I32, s.shape, 0)
    rank = jnp.full(s.shape, PEER_TOPK, I32)
    tops = []
    for kth in range(PEER_TOPK):
        m = jnp.max(s, axis=0, keepdims=True)
        idx = jnp.min(jnp.where(s == m, n, PEER_NKEYS), axis=0, keepdims=True)
        sel = n == idx
        rank = jnp.where(sel, kth, rank)
        s = jnp.where(sel, NEG_INF, s)
        tops.append(m)
    return rank, jnp.concatenate(tops, axis=0)


def _pair_top16(top1, top2, rank1):
    L = top1.shape[1]
    sub = lax.broadcasted_iota(I32, (8, L), 0)
    cands, flats = [], []
    for i in range(8):
        for jb in ((0, 8) if i == 0 else (0,)):
            width = PEER_TOPK // (i + 1) - jb
            cnd = top1[i:i + 1, :] + top2[jb:jb + 8, :]
            cands.append(cnd if width >= 8 else jnp.where(sub < width, cnd, NEG_INF))
            flats.append(sub + (PEER_TOPK * i + jb))
    cands.append(top1[8:16, :] + top2[0:1, :])
    flats.append((sub + 8) * PEER_TOPK)
    cand = jnp.concatenate(cands, axis=0)
    flat = jnp.concatenate(flats, axis=0)
    big = PEER_TOPK * PEER_TOPK
    m0 = top1[0:1, :] + top2[0:1, :]
    z = jnp.zeros((1, L), F32)
    count = jnp.zeros(rank1.shape, F32)
    for _ in range(PEER_TOPK):
        m = jnp.max(cand, axis=0, keepdims=True)
        idx = jnp.min(jnp.where(cand == m, flat, big), axis=0, keepdims=True)
        cand = jnp.where(flat == idx, NEG_INF, cand)
        z = z + jnp.exp(m - m0)
        count = count + jnp.where(rank1 == (idx >> 4), 1.0, 0.0)
    return count, z


def _peer_route_kernel(h2_ref, wq_ref, keys_ref, p1_out, je_out, p2_out, r2_out, s_s):
    tt = h2_ref.shape[0]
    qt = _dot_nt(wq_ref[...], h2_ref[...]).astype(BF16)
    s_s[0] = _dot(keys_ref[0, 0], qt[:PEER_HALF, :])
    s_s[1] = _dot(keys_ref[0, 1], qt[PEER_HALF:, :])

    def lane_group(gi, carry):
        ls = pl.ds(pl.multiple_of(gi * LANES, LANES), LANES)
        s1 = s_s[0, :, ls]
        s2 = s_s[1, :, ls]
        rank1, top1 = _top16(s1)
        rank2, top2 = _top16(s2)
        count, z = _pair_top16(top1, top2, rank1)
        p1 = jnp.where(rank1 < PEER_TOPK, jnp.exp(s1 - top1[0:1, :]), 0.0) / z
        p2 = jnp.where(rank2 < PEER_TOPK, jnp.exp(s2 - top2[0:1, :]), 0.0)
        p1_out[0, :, ls] = p1
        je_out[0, :, ls] = count
        p2_out[0, :, ls] = p2.astype(p2_out.dtype)
        r2_out[0, :, ls] = rank2.astype(F32).astype(r2_out.dtype)
        return carry

    lax.fori_loop(0, tt // LANES, lane_group, 0)


def _peer_route(h2, peer_wq, peer_keys, *, tt):
    S, D = h2.shape
    wqt = peer_wq.T.astype(BF16)
    keys = peer_keys.astype(BF16)
    out_spec = pl.BlockSpec((1, PEER_NKEYS, tt), lambda i, h: (h, 0, i))
    shp = lambda dt: jax.ShapeDtypeStruct((PEER_HEADS, PEER_NKEYS, S), dt)
    return pl.pallas_call(
        _peer_route_kernel,
        grid=(S // tt, PEER_HEADS),
        in_specs=[pl.BlockSpec((tt, D), lambda i, h: (i, 0)),
                  pl.BlockSpec((2 * PEER_HALF, D), lambda i, h: (h, 0)),
                  pl.BlockSpec((1, 2, PEER_NKEYS, PEER_HALF), lambda i, h: (h, 0, 0, 0))],
        out_specs=[out_spec] * 4,
        out_shape=[shp(F32), shp(F32), shp(BF16), shp(BF16)],
        scratch_shapes=[pltpu.VMEM((2, PEER_NKEYS, tt), F32)],
        compiler_params=pltpu.CompilerParams(
            dimension_semantics=("arbitrary", "arbitrary"), vmem_limit_bytes=VMEM_LIMIT),
        name="peer_route",
    )(h2, wqt, keys)


def _gelu_tanh(x):
    return 0.5 * x * (1.0 + jnp.tanh(0.7978845608028654 * (x + 0.044715 * (x * x * x))))


def _peer_ffn_kernel(h2_ref, u_ref, vt_ref, p1_ref, je_ref, p2_ref, r2_ref, x1_ref, nfin_ref,
                     out_ref, acc_s, ga_s):
    j = pl.program_id(1)
    te = u_ref.shape[0]
    tt = h2_ref.shape[0]

    @pl.when(j == 0)
    def _():
        acc_s[...] = jnp.zeros(acc_s.shape, F32)

    act = _gelu_tanh(_dot_nt(u_ref[...], h2_ref[...]))
    for e1l in range(te // PEER_NKEYS):
        e1 = j * (te // PEER_NKEYS) + e1l
        gate = None
        for h in range(PEER_HEADS):
            p1 = p1_ref[h, pl.ds(e1, 1), :].astype(BF16)
            pre = je_ref[h, pl.ds(e1, 1), :].astype(BF16)
            term = jnp.where(r2_ref[h] < pre, p2_ref[h] * p1, jnp.zeros((), BF16))
            gate = term if gate is None else gate + term
        rows = slice(e1l * PEER_NKEYS, (e1l + 1) * PEER_NKEYS)
        ga_s[rows, :] = gate * act[rows, :].astype(BF16)
    acc_s[...] += _dot(vt_ref[...], ga_s[...])

    @pl.when(j == pl.num_programs(1) - 1)
    def _():
        for cb in range(tt // LANES):
            rows = slice(cb * LANES, (cb + 1) * LANES)
            x = x1_ref[rows, :] + acc_s[:, rows].T
            y = x * lax.rsqrt(jnp.mean(x * x, axis=-1, keepdims=True) + RMS_EPS) * nfin_ref[...]
            out_ref[rows, :] = y.astype(out_ref.dtype)


def _peer_ffn(h2, x1, p1, je, p2, r2, peer_u, peer_v, norm_final, *, tt, te):
    S, D = h2.shape
    E = peer_u.shape[0]
    ub = peer_u.astype(BF16)
    vtb = peer_v.T.astype(BF16)
    tok = lambda dt_rows: pl.BlockSpec((PEER_HEADS, PEER_NKEYS, tt), lambda i, j: (0, 0, i))
    return pl.pallas_call(
        _peer_ffn_kernel,
        grid=(S // tt, E // te),
        in_specs=[pl.BlockSpec((tt, D), lambda i, j: (i, 0)),
                  pl.BlockSpec((te, D), lambda i, j: (j, 0)),
                  pl.BlockSpec((D, te), lambda i, j: (0, j)),
                  tok(F32), tok(F32), tok(BF16), tok(BF16),
                  pl.BlockSpec((tt, D), lambda i, j: (i, 0)),
                  pl.BlockSpec((1, D), lambda i, j: (0, 0))],
        out_specs=pl.BlockSpec((tt, D), lambda i, j: (i, 0)),
        out_shape=jax.ShapeDtypeStruct((S, D), F32),
        scratch_shapes=[pltpu.VMEM((D, tt), F32), pltpu.VMEM((te, tt), BF16)],
        compiler_params=pltpu.CompilerParams(
            dimension_semantics=("arbitrary", "arbitrary"), vmem_limit_bytes=VMEM_LIMIT),
        name="peer_ffn",
    )(h2, ub, vtb, p1, je, p2, r2, x1, norm_final.reshape(1, -1).astype(F32))


def _tile(n, pref):
    t = min(n, pref)
    assert n % t == 0, (n, t)
    return t


def kernel(x, norm_mix, w_in, mu_rkv, mu_lora, w0, w1, w2, a0, a1, a2, g1, g2, k_k, k_a, r_k,
           gn_w, gn_b, conv_w, w_out, norm_ffn, peer_wq, peer_keys, peer_u, peer_v, norm_final):
    B, S, D = x.shape
    assert B == 1 and S % SCAN_CHUNK == 0
    xs = x.reshape(S, D)
    for l in range(w_in.shape[0]):
        r, lw, k, v, kk, a, g, y_conv = _mix_in(
            xs, norm_mix[l], w_in[l], mu_rkv[l], mu_lora[l], w0[l], w1[l], w2[l], a0[l], a1[l], a2[l],
            g1[l], g2[l], k_k[l], k_a[l], conv_w[l], tm=_tile(S, 512), tc=256)
        y_rwkv = _rwkv_scan(r, lw, k, v, kk, a, g, gn_w[l], gn_b[l], r_k[l])
        x1, h2 = _mix_out(xs, y_rwkv, y_conv, w_out[l], norm_ffn[l], tm=_tile(S, 512))
        p1, je, p2, r2 = _peer_route(h2, peer_wq[l], peer_keys[l], tt=_tile(S, 256))
        assert l == w_in.shape[0] - 1
        xs = _peer_ffn(h2, x1, p1, je, p2, r2, peer_u[l], peer_v[l], norm_final,
                       tt=_tile(S, 512), te=512)
    return xs.reshape(B, S, D)
```

```python
import functools

import jax
import jax.numpy as jnp
from jax import lax
from jax.experimental import pallas as pl
from jax.experimental.pallas import tpu as pltpu

F32 = jnp.float32
BF16 = jnp.bfloat16
I32 = jnp.int32

LANES = 128
RWKV_HEAD = 64
HEADS_PER_GROUP = LANES // RWKV_HEAD
SCAN_CHUNK = 64
LORA_DECAY = 64
LORA_AAA = 64
LORA_GATE = 160
CONV_WIDTH = 3
RMS_EPS = 1e-6
GN_EPS = 64e-5
PEER_HEADS = 8
PEER_NKEYS = 128
PEER_HALF = 256
PEER_TOPK = 16
VMEM_LIMIT = 56 * 1024 * 1024

NT_DIMS = (((1,), (1,)), ((), ()))
TN_DIMS = (((0,), (0,)), ((), ()))
NEG_INF = float("-inf")


def _dot(a, b):
    return jnp.dot(a, b, preferred_element_type=F32)


def _dot_nt(a, b):
    return lax.dot_general(a, b, NT_DIMS, preferred_element_type=F32)


def _dot_tn(a, b):
    return lax.dot_general(a, b, TN_DIMS, preferred_element_type=F32)


def _split3(t):
    hi = t.astype(BF16)
    r1 = t - hi.astype(F32)
    mid = r1.astype(BF16)
    lo = (r1 - mid.astype(F32)).astype(BF16)
    return hi, mid, lo


def _head_sum(t, ones_bd):
    hi, mid, lo = _split3(t)
    return _dot(hi, ones_bd) + _dot(mid, ones_bd) + _dot(lo, ones_bd)


def _block_diag_ones(n):
    r = lax.broadcasted_iota(I32, (n, n), 0) >> 6
    c = lax.broadcasted_iota(I32, (n, n), 1) >> 6
    return (r == c).astype(BF16)


def _shift_rows(t, carry, n):
    rolled = pltpu.roll(t, shift=n, axis=0)
    row = lax.broadcasted_iota(I32, t.shape, 0)
    out = jnp.where(row == 0, carry[8 - n:9 - n, :], rolled)
    if n == 2:
        out = jnp.where(row == 1, carry[7:8, :], out)
    return out


def _mix_in_kernel(x_ref, nm_ref, mul_ref, w1_ref, a1_ref, g1_ref,
                   wr_ref, wk_ref, wv_ref, wb_ref, wc_ref, wx_ref,
                   mur_ref, muk_ref, muv_ref, w0_ref, w2_ref, a0_ref, a2_ref, g2_ref,
                   kkw_ref, kaw_ref, cw_ref,
                   r_out, lw_out, k_out, v_out, kk_out, a_out, g_out, yc_out,
                   h_s, hw_s, ha_s, hg_s, hlast_s, plast_s, ulast_s):
    i = pl.program_id(0)
    j = pl.program_id(1)
    tm = x_ref.shape[0]

    @pl.when(i == 0)
    def _():
        plast_s[j] = jnp.zeros(plast_s.shape[1:], F32)
        ulast_s[j] = jnp.zeros(ulast_s.shape[1:], F32)

    @pl.when(jnp.logical_and(i == 0, j == 0))
    def _():
        hlast_s[...] = jnp.zeros(hlast_s.shape, F32)

    @pl.when(j == 0)
    def _():
        x = x_ref[...]
        h = x * lax.rsqrt(jnp.mean(x * x, axis=-1, keepdims=True) + RMS_EPS) * nm_ref[...]
        h_s[...] = h.astype(BF16)
        dh = _shift_rows(h, hlast_s[...], 1) - h
        hlast_s[...] = h[tm - 8:, :]
        xw = (h + dh * mul_ref[0:1, :]).astype(BF16)
        xa = (h + dh * mul_ref[1:2, :]).astype(BF16)
        xg = (h + dh * mul_ref[2:3, :]).astype(BF16)
        hw_s[...] = jnp.tanh(_dot(xw, w1_ref[...])).astype(BF16)
        ha_s[...] = _dot(xa, a1_ref[...]).astype(BF16)
        hg_s[...] = jax.nn.sigmoid(_dot(xg, g1_ref[...])).astype(BF16)

    hb = h_s[...]
    tc = wr_ref.shape[1]
    plast = plast_s[j]

    def shifted_proj(w_ref, mu_ref, slot):
        p = _dot(hb, w_ref[...])
        prev = _shift_rows(p, plast[:, slot * tc:(slot + 1) * tc], 1)
        plast_s[j, :, slot * tc:(slot + 1) * tc] = p[tm - 8:, :]
        return p + (prev - p) * mu_ref[...]

    r = shifted_proj(wr_ref, mur_ref, 0)
    k = shifted_proj(wk_ref, muk_ref, 1)
    v = shifted_proj(wv_ref, muv_ref, 2)

    u = _dot(hb, wc_ref[...]) * _dot(hb, wx_ref[...])
    ulast = ulast_s[j]
    conv = (_shift_rows(u, ulast, 2) * cw_ref[0:1, :] + _shift_rows(u, ulast, 1) * cw_ref[1:2, :]
            + u * cw_ref[2:3, :])
    ulast_s[j] = u[tm - 8:, :]
    yc_out[...] = (_dot(hb, wb_ref[...]) * conv).astype(yc_out.dtype)

    z = w0_ref[...] + _dot(hw_s[...], w2_ref[...])
    w_log = -jax.nn.softplus(-z) - 0.5
    lw_out[...] = -jnp.exp(w_log)
    a = jax.nn.sigmoid(a0_ref[...] + _dot(ha_s[...], a2_ref[...]))
    g_out[...] = _dot(hg_s[...], g2_ref[...])

    kk = k * kkw_ref[...]
    ss = _head_sum(kk * kk, _block_diag_ones(tc))
    kk_out[...] = kk / jnp.maximum(jnp.sqrt(ss), 1e-12)
    k_out[...] = k * (1.0 + (a - 1.0) * kaw_ref[...])
    r_out[...] = r
    v_out[...] = v
    a_out[...] = a


def _mix_in(x2, norm_mix, w_in, mu_rkv, mu_lora, w0, w1, w2, a0, a1, a2, g1, g2, k_k, k_a, conv_w,
            *, tm, tc):
    S, D = x2.shape
    C = w0.shape[-1]
    nj = C // tc
    wb = w_in.astype(BF16)
    row = lambda t: t.reshape(1, -1).astype(F32)

    def wspec(slot):
        return pl.BlockSpec((D, tc), lambda i, j, slot=slot: (0, slot * nj + j))

    def cspec(rows=1):
        return pl.BlockSpec((rows, tc), lambda i, j: (0, j))

    def muspec(slot):
        return pl.BlockSpec((1, tc), lambda i, j, slot=slot: (0, slot * nj + j))

    full = lambda shape: pl.BlockSpec(shape, lambda i, j: (0,) * len(shape))
    mu_row = row(mu_rkv)
    in_specs = [
        pl.BlockSpec((tm, D), lambda i, j: (i, 0)),
        full((1, D)),
        full((3, D)),
        full((D, LORA_DECAY)), full((D, LORA_AAA)), full((D, LORA_GATE)),
        wspec(0), wspec(1), wspec(2), wspec(3), wspec(4), wspec(5),
        muspec(0), muspec(1), muspec(2),
        cspec(), pl.BlockSpec((LORA_DECAY, tc), lambda i, j: (0, j)),
        cspec(), pl.BlockSpec((LORA_AAA, tc), lambda i, j: (0, j)),
        pl.BlockSpec((LORA_GATE, tc), lambda i, j: (0, j)),
        cspec(), cspec(), cspec(CONV_WIDTH),
    ]
    out_spec = pl.BlockSpec((tm, tc), lambda i, j: (i, j))
    f32_out = jax.ShapeDtypeStruct((S, C), F32)
    outs = pl.pallas_call(
        _mix_in_kernel,
        grid=(S // tm, nj),
        in_specs=in_specs,
        out_specs=[out_spec] * 8,
        out_shape=[f32_out] * 7 + [jax.ShapeDtypeStruct((S, C), BF16)],
        scratch_shapes=[
            pltpu.VMEM((tm, D), BF16),
            pltpu.VMEM((tm, LORA_DECAY), BF16),
            pltpu.VMEM((tm, LORA_AAA), BF16),
            pltpu.VMEM((tm, LORA_GATE), BF16),
            pltpu.VMEM((8, D), F32),
            pltpu.VMEM((nj, 8, 3 * tc), F32),
            pltpu.VMEM((nj, 8, tc), F32),
        ],
        compiler_params=pltpu.CompilerParams(
            dimension_semantics=("arbitrary", "arbitrary"), vmem_limit_bytes=VMEM_LIMIT),
        name="mix_in",
    )(x2, row(norm_mix), mu_lora.astype(F32), w1.astype(BF16), a1.astype(BF16), g1.astype(BF16),
      wb, wb, wb, wb, wb, wb, mu_row, mu_row, mu_row,
      row(w0), w2.astype(BF16), row(a0), a2.astype(BF16), g2.astype(BF16),
      row(k_k), row(k_a), conv_w.astype(F32))
    return outs


def _rwkv_kernel(r_ref, lw_ref, k_ref, v_ref, kk_ref, a_ref, g_ref, gnw_ref, gnb_ref, rk_ref,
                 y_ref, s_ref, *, group):
    c = pl.program_id(0)
    T = SCAN_CHUNK
    G = group
    heads = G // RWKV_HEAD
    groups = range(r_ref.shape[1] // G)
    R = heads * T

    @pl.when(c == 0)
    def _():
        s_ref[...] = jnp.zeros(s_ref.shape, F32)

    lane_head = lax.broadcasted_iota(I32, (T, G), 1) >> 6
    row = lax.broadcasted_iota(I32, (R, R), 0)
    col = lax.broadcasted_iota(I32, (R, R), 1)
    strict = (col & (T - 1)) < (row & (T - 1))
    incl = (col & (T - 1)) <= (row & (T - 1))
    eye = (row == col).astype(F32)
    ones_bd = _block_diag_ones(G)
    tri = (lax.broadcasted_iota(I32, (T, T), 0) >= lax.broadcasted_iota(I32, (T, T), 1)).astype(BF16)

    def stack(t):
        return jnp.concatenate([jnp.where(lane_head == h, t, 0.0) for h in range(heads)],
                               axis=0).astype(BF16)

    def head_sum(t):
        parts = _dot(jnp.concatenate(_split3(t), axis=0), ones_bd)
        return parts[:T] + parts[T:2 * T] + parts[2 * T:]

    sls = [slice(p * G, (p + 1) * G) for p in groups]

    cs = []
    for sl in sls:
        parts = _dot(tri, jnp.concatenate(_split3(lw_ref[:, sl]), axis=1))
        cs.append(parts[:, :G] + parts[:, G:2 * G] + parts[:, 2 * G:])

    xs, ys, vs, yo_t, decay_col = [], [], [], [], []
    for p, sl in enumerate(sls):
        lw = lw_ref[:, sl]
        cs_end = cs[p][T - 1:T, :]
        grow = jnp.exp(-cs[p])
        decay_in = jnp.exp(cs[p])
        decay_out = jnp.exp(cs_end - cs[p])
        kk = kk_ref[:, sl]
        beta = kk * a_ref[:, sl]
        kv = k_ref[:, sl]
        xs.append(jnp.concatenate([stack(-kk * jnp.exp(cs[p] - lw)),
                                   stack(r_ref[:, sl] * decay_in)], axis=0))
        ys.append(jnp.concatenate([stack(beta * grow), stack(kv * grow)], axis=0))
        vs.append(stack(v_ref[:, sl]))
        yo_t.append(jnp.concatenate([stack(beta * decay_out), stack(kv * decay_out)], axis=0).T)
        decay_col.append(decay_in.T[:, T - 1:T])

    quad = [_dot_nt(xs[p], ys[p]) for p in groups]
    l_ab = [jnp.where(strict, q[:R, :R], 0.0) for q in quad]
    l_ak = [jnp.where(strict, q[:R, R:], 0.0).astype(BF16) for q in quad]
    m_r = [jnp.concatenate([jnp.where(incl, q[R:, :R], 0.0), jnp.where(incl, q[R:, R:], 0.0)],
                           axis=1).astype(BF16) for q in quad]

    def square(ms):
        return [_dot(m.astype(BF16), m.astype(BF16)) for m in ms]

    def pair_product(lo_pows, hi_pows):
        out = []
        for lo_p, hi_p in zip(lo_pows, hi_pows):
            f = eye + lo_p
            out.append(f + _dot(f.astype(BF16), hi_p.astype(BF16)))
        return out

    n2 = square(l_ab)
    n4 = square(n2)
    a1 = pair_product(l_ab, n2)
    n8 = square(n4)
    a2 = pair_product(n4, n8)
    n16 = square(n8)
    n32 = square(n16)
    a3 = pair_product(n16, n32)
    t12 = [_dot(x.astype(BF16), y.astype(BF16)) for x, y in zip(a1, a2)]
    tinv = [_dot(x.astype(BF16), y.astype(BF16)).astype(BF16) for x, y in zip(t12, a3)]

    rhs0 = [_dot(l_ak[p], vs[p]) for p in groups]
    s0 = [s_ref[p] for p in groups]
    xs_s0 = [_dot(xs[p], s0[p].astype(BF16)) for p in groups]
    u = [_dot(tinv[p], (rhs0[p] + xs_s0[p][:R]).astype(BF16)).astype(BF16) for p in groups]
    uv = [jnp.concatenate([u[p], vs[p]], axis=0) for p in groups]
    yst = [_dot(m_r[p], uv[p]) + xs_s0[p][R:] for p in groups]
    for p in groups:
        s_ref[p] = s0[p] * decay_col[p] + _dot(yo_t[p], uv[p])

    inv_n = 1.0 / RWKV_HEAD
    rk_sum = [head_sum(r_ref[:, sl] * k_ref[:, sl] * rk_ref[:, sl]) for sl in sls]
    ysum = []
    for p in groups:
        y = yst[p][:T]
        for h in range(1, heads):
            y = y + yst[p][h * T:(h + 1) * T]
        ysum.append(y)
    dev = [ysum[p] - head_sum(ysum[p]) * inv_n for p in groups]
    var = [head_sum(d * d) * inv_n for d in dev]
    for p, sl in enumerate(sls):
        yn = dev[p] * lax.rsqrt(var[p] + GN_EPS) * gnw_ref[:, sl] + gnb_ref[:, sl]
        y_ref[:, sl] = ((yn + rk_sum[p] * v_ref[:, sl]) * g_ref[:, sl]).astype(y_ref.dtype)


def _rwkv_scan(r, lw, k, v, kk, a, g, gn_w, gn_b, r_k, *, group):
    S, C = r.shape
    blk = pl.BlockSpec((SCAN_CHUNK, C), lambda c: (c, 0))
    par = pl.BlockSpec((1, C), lambda c: (0, 0))
    row = lambda t: t.reshape(1, -1).astype(F32)
    return pl.pallas_call(
        functools.partial(_rwkv_kernel, group=group),
        grid=(S // SCAN_CHUNK,),
        in_specs=[blk] * 7 + [par] * 3,
        out_specs=blk,
        out_shape=jax.ShapeDtypeStruct((S, C), BF16),
        scratch_shapes=[pltpu.VMEM((C // group, group, group), F32)],
        compiler_params=pltpu.CompilerParams(
            dimension_semantics=("arbitrary",), vmem_limit_bytes=VMEM_LIMIT),
        name="rwkv_scan",
    )(r, lw, k, v, kk, a, g, row(gn_w), row(gn_b), row(r_k))


def _mix_out_kernel(x_ref, ya_ref, yb_ref, wa_ref, wb_ref, nf_ref, x1_out, h2t_out):
    x1 = x_ref[...] + _dot(ya_ref[...], wa_ref[...]) + _dot(yb_ref[...], wb_ref[...])
    x1_out[...] = x1
    h2 = x1 * lax.rsqrt(jnp.mean(x1 * x1, axis=-1, keepdims=True) + RMS_EPS) * nf_ref[...]
    h2t_out[...] = h2.astype(h2t_out.dtype).T


def _mix_out(x2, y_rwkv, y_conv, w_out, norm_ffn, *, tm):
    S, D = x2.shape
    C = y_rwkv.shape[1]
    wo = w_out.astype(BF16)
    rows = lambda n: pl.BlockSpec((tm, n), lambda i: (i, 0))
    return pl.pallas_call(
        _mix_out_kernel,
        grid=(S // tm,),
        in_specs=[rows(D), rows(C), rows(C),
                  pl.BlockSpec((C, D), lambda i: (0, 0)), pl.BlockSpec((C, D), lambda i: (1, 0)),
                  pl.BlockSpec((1, D), lambda i: (0, 0))],
        out_specs=[rows(D), pl.BlockSpec((D, tm), lambda i: (0, i))],
        out_shape=[jax.ShapeDtypeStruct((S, D), F32), jax.ShapeDtypeStruct((D, S), BF16)],
        compiler_params=pltpu.CompilerParams(
            dimension_semantics=("arbitrary",), vmem_limit_bytes=VMEM_LIMIT),
        name="mix_out",
    )(x2, y_rwkv, y_conv, wo, wo, norm_ffn.reshape(1, -1).astype(F32))


def _top16(s):
    n = lax.broadcasted_iota(I32, s.shape, 0)
    rank = jnp.full(s.shape, PEER_TOPK, I32)
    tops = []
    for kth in range(PEER_TOPK):
        m = jnp.max(s, axis=0, keepdims=True)
        idx = jnp.min(jnp.where(s == m, n, PEER_NKEYS), axis=0, keepdims=True)
        sel = n == idx
        rank = jnp.where(sel, kth, rank)
        s = jnp.where(sel, NEG_INF, s)
        tops.append(m)
    return rank, jnp.concatenate(tops, axis=0)


def _pair_top16(top1, top2, rank1):
    L = top1.shape[1]
    sub = lax.broadcasted_iota(I32, (8, L), 0)
    cands, flats = [], []
    for i in range(8):
        for jb in ((0, 8) if i == 0 else (0,)):
            width = PEER_TOPK // (i + 1) - jb
            cnd = top1[i:i + 1, :] + top2[jb:jb + 8, :]
            cands.append(cnd if width >= 8 else jnp.where(sub < width, cnd, NEG_INF))
            flats.append(sub + (PEER_TOPK * i + jb))
    cands.append(top1[8:16, :] + top2[0:1, :])
    flats.append((sub + 8) * PEER_TOPK)
    cand = jnp.concatenate(cands, axis=0)
    flat = jnp.concatenate(flats, axis=0)
    big = PEER_TOPK * PEER_TOPK
    m0 = top1[0:1, :] + top2[0:1, :]
    z = jnp.zeros((1, L), F32)
    count = jnp.zeros(rank1.shape, F32)
    for _ in range(PEER_TOPK):
        m = jnp.max(cand, axis=0, keepdims=True)
        idx = jnp.min(jnp.where(cand == m, flat, big), axis=0, keepdims=True)
        cand = jnp.where(flat == idx, NEG_INF, cand)
        z = z + jnp.exp(m - m0)
        count = count + jnp.where(rank1 == (idx >> 4), 1.0, 0.0)
    return count, z


def _peer_route_kernel(h2t_ref, wq_ref, keys_ref, p1_out, je_out, p2_out, r2_out, s_s):
    tt = h2t_ref.shape[1]
    qt = _dot(wq_ref[...], h2t_ref[...]).astype(BF16)
    s_s[0] = _dot(keys_ref[0, 0], qt[:PEER_HALF, :])
    s_s[1] = _dot(keys_ref[0, 1], qt[PEER_HALF:, :])

    def lane_group(gi, carry):
        ls = pl.ds(pl.multiple_of(gi * LANES, LANES), LANES)
        s1 = s_s[0, :, ls]
        s2 = s_s[1, :, ls]
        rank1, top1 = _top16(s1)
        rank2, top2 = _top16(s2)
        count, z = _pair_top16(top1, top2, rank1)
        p1 = jnp.where(rank1 < PEER_TOPK, jnp.exp(s1 - top1[0:1, :]), 0.0) / z
        p2 = jnp.where(rank2 < PEER_TOPK, jnp.exp(s2 - top2[0:1, :]), 0.0)
        p1_out[0, :, ls] = p1
        je_out[0, :, ls] = count
        p2_out[0, :, ls] = p2.astype(p2_out.dtype)
        r2_out[0, :, ls] = rank2.astype(F32).astype(r2_out.dtype)
        return carry

    lax.fori_loop(0, tt // LANES, lane_group, 0)


def _peer_route(h2t, peer_wq, peer_keys, *, tt):
    D, S = h2t.shape
    wqt = peer_wq.T.astype(BF16)
    keys = peer_keys.astype(BF16)
    out_spec = pl.BlockSpec((1, PEER_NKEYS, tt), lambda i, h: (h, 0, i))
    shp = lambda dt: jax.ShapeDtypeStruct((PEER_HEADS, PEER_NKEYS, S), dt)
    return pl.pallas_call(
        _peer_route_kernel,
        grid=(S // tt, PEER_HEADS),
        in_specs=[pl.BlockSpec((D, tt), lambda i, h: (0, i)),
                  pl.BlockSpec((2 * PEER_HALF, D), lambda i, h: (h, 0)),
                  pl.BlockSpec((1, 2, PEER_NKEYS, PEER_HALF), lambda i, h: (h, 0, 0, 0))],
        out_specs=[out_spec] * 4,
        out_shape=[shp(F32), shp(F32), shp(BF16), shp(BF16)],
        scratch_shapes=[pltpu.VMEM((2, PEER_NKEYS, tt), F32)],
        compiler_params=pltpu.CompilerParams(
            dimension_semantics=("arbitrary", "arbitrary"), vmem_limit_bytes=VMEM_LIMIT),
        name="peer_route",
    )(h2t, wqt, keys)


def _gelu_tanh(x):
    return 0.5 * x * (1.0 + jnp.tanh(0.7978845608028654 * (x + 0.044715 * (x * x * x))))


def _peer_ffn_kernel(h2t_ref, u_ref, vt_ref, p1_ref, je_ref, p2_ref, r2_ref, x1_ref, nfin_ref,
                     out_ref, acc_s, ga_a, ga_b, *, n_blocks, lane_chunk):
    j = pl.program_id(1)
    te = u_ref.shape[0]
    tt = h2t_ref.shape[1]
    e1_per_block = te // PEER_NKEYS

    @pl.when(j == 0)
    def _():
        acc_s[...] = jnp.zeros(acc_s.shape, F32)
        ga_b[...] = jnp.zeros(ga_b.shape, ga_b.dtype)

    block = jnp.minimum(j, n_blocks - 1)

    def step(ga_new, ga_old):
        for c in range(tt // lane_chunk):
            ls = slice(c * lane_chunk, (c + 1) * lane_chunk)
            act = _dot(u_ref[...], h2t_ref[:, ls])
            acc_s[:, ls] += _dot(vt_ref[...], ga_old[:, ls])
            for e1l in range(e1_per_block):
                e1 = block * e1_per_block + e1l
                gate = None
                for h in range(PEER_HEADS):
                    p1 = p1_ref[h, pl.ds(e1, 1), ls].astype(BF16)
                    pre = je_ref[h, pl.ds(e1, 1), ls].astype(BF16)
                    term = jnp.where(r2_ref[h, :, ls] < pre, p2_ref[h, :, ls] * p1, jnp.zeros((), BF16))
                    gate = term if gate is None else gate + term
                rows = slice(e1l * PEER_NKEYS, (e1l + 1) * PEER_NKEYS)
                ga_new[rows, ls] = gate * _gelu_tanh(act[rows, :]).astype(BF16)

    @pl.when((j & 1) == 0)
    def _():
        step(ga_a, ga_b)

    @pl.when((j & 1) == 1)
    def _():
        step(ga_b, ga_a)

    @pl.when(j == n_blocks)
    def _():
        for cb in range(tt // LANES):
            rows = slice(cb * LANES, (cb + 1) * LANES)
            x = x1_ref[rows, :] + acc_s[:, rows].T
            y = x * lax.rsqrt(jnp.mean(x * x, axis=-1, keepdims=True) + RMS_EPS) * nfin_ref[...]
            out_ref[rows, :] = y.astype(out_ref.dtype)


def _peer_ffn(h2t, x1, p1, je, p2, r2, peer_u, peer_v, norm_final, *, tt, te):
    D, S = h2t.shape
    E = peer_u.shape[0]
    nb = E // te
    ub = peer_u.astype(BF16)
    vtb = peer_v.T.astype(BF16)
    tok = pl.BlockSpec((PEER_HEADS, PEER_NKEYS, tt), lambda i, j: (0, 0, i))
    return pl.pallas_call(
        functools.partial(_peer_ffn_kernel, n_blocks=nb, lane_chunk=256),
        grid=(S // tt, nb + 1),
        in_specs=[pl.BlockSpec((D, tt), lambda i, j: (0, i)),
                  pl.BlockSpec((te, D), lambda i, j: (jnp.minimum(j, nb - 1), 0)),
                  pl.BlockSpec((D, te), lambda i, j: (0, jnp.maximum(j - 1, 0))),
                  tok, tok, tok, tok,
                  pl.BlockSpec((tt, D), lambda i, j: (i, 0)),
                  pl.BlockSpec((1, D), lambda i, j: (0, 0))],
        out_specs=pl.BlockSpec((tt, D), lambda i, j: (i, 0)),
        out_shape=jax.ShapeDtypeStruct((S, D), F32),
        scratch_shapes=[pltpu.VMEM((D, tt), F32), pltpu.VMEM((te, tt), BF16), pltpu.VMEM((te, tt), BF16)],
        compiler_params=pltpu.CompilerParams(
            dimension_semantics=("arbitrary", "arbitrary"), vmem_limit_bytes=VMEM_LIMIT),
        name="peer_ffn",
    )(h2t, ub, vtb, p1, je, p2, r2, x1, norm_final.reshape(1, -1).astype(F32))


def _tile(n, pref):
    t = min(n, pref)
    assert n % t == 0, (n, t)
    return t


def kernel(x, norm_mix, w_in, mu_rkv, mu_lora, w0, w1, w2, a0, a1, a2, g1, g2, k_k, k_a, r_k,
           gn_w, gn_b, conv_w, w_out, norm_ffn, peer_wq, peer_keys, peer_u, peer_v, norm_final):
    B, S, D = x.shape
    assert B == 1 and S % SCAN_CHUNK == 0 and w_in.shape[0] == 1
    xs = x.reshape(S, D)
    r, lw, k, v, kk, a, g, y_conv = _mix_in(
        xs, norm_mix[0], w_in[0], mu_rkv[0], mu_lora[0], w0[0], w1[0], w2[0], a0[0], a1[0], a2[0],
        g1[0], g2[0], k_k[0], k_a[0], conv_w[0], tm=_tile(S, 512), tc=256)
    y_rwkv = _rwkv_scan(r, lw, k, v, kk, a, g, gn_w[0], gn_b[0], r_k[0], group=128)
    x1, h2t = _mix_out(xs, y_rwkv, y_conv, w_out[0], norm_ffn[0], tm=_tile(S, 512))
    p1, je, p2, r2 = _peer_route(h2t, peer_wq[0], peer_keys[0], tt=_tile(S, 256))
    out = _peer_ffn(h2t, x1, p1, je, p2, r2, peer_u[0], peer_v[0], norm_final,
                    tt=_tile(S, 512), te=512)
    return out.reshape(B, S, D)
```
